```python
import jax, jax.numpy as jnp
from jax import lax
import numpy as np

D_MODEL = 1024
BATCH = 16
SEQ = 256
DEPTH = 2
DEC_BATCH = 4
DEC_SEQ = 1024
PAST_LEN = 256

GRID_W = 64
EPS = 1e-6
GLA_HEADS = 4
GLA_DK = 64
GLA_DV = 128
GLA_LOWRANK = 16
GLA_TAU = 16.0
GLA_CHUNK = 64
GLA_QK = GLA_HEADS * GLA_DK
GLA_V = GLA_HEADS * GLA_DV
POOL_GROUPS = 4
POOL_GROUP_DIM = 128
POOL_WINDOWS = (2, 4, 8, 16)
POOL_DIM = POOL_GROUPS * POOL_GROUP_DIM
IN0_DIM = 2 * GLA_QK + 2 * GLA_V + 2 * GLA_LOWRANK + POOL_DIM
MIX0_DIM = GLA_V + POOL_DIM
ATT_HEADS = 16
ATT_KV_HEADS = 4
ATT_HD = 64
ATT_WINDOW = 128
ATT_BLOCK = 128
ROPE_THETA = 10000.0
IN1_DIM = (ATT_HEADS + 2 * ATT_KV_HEADS) * ATT_HD
PEER_HEADS = 8
PEER_NKEYS = 128
PEER_N = PEER_NKEYS * PEER_NKEYS
PEER_QDIM = 256
PEER_HALF = PEER_QDIM // 2
PEER_TOPK = 16
PEER_TOK_BLOCK = 128

kernel_name = 'hybrid_flow_ctx_prefix_step'


def rms_norm(x, gain):
    xf = x.astype(jnp.float32)
    y = xf * lax.rsqrt(jnp.mean(xf * xf, axis=-1, keepdims=True) + EPS)
    return (y * gain.astype(jnp.float32)).astype(x.dtype)


def split_cols(t, sizes):
    return jnp.split(t, np.cumsum(sizes)[:-1].tolist(), axis=-1)


def modulation(cond, w_mod, b_mod):
    m = jax.nn.silu(cond) @ w_mod + b_mod
    return jnp.split(m[:, None, :], 6, axis=-1)


def modulate(x, gain, shift, scale):
    return rms_norm(x, gain) * (1.0 + scale) + shift


def gla_chunked(q, k, v, log_a, s0):
    B, S, H, DK = q.shape
    DV = v.shape[-1]
    C = GLA_CHUNK
    n = S // C

    def blocks(t):
        return t.astype(jnp.float32).reshape(B, n, C, H, t.shape[-1]).transpose(1, 0, 3, 2, 4)

    qc, kc, vc, ac = blocks(q), blocks(k), blocks(v), blocks(log_a)
    b = jnp.cumsum(ac, axis=3)
    b_tot = b[:, :, :, -1, :]
    causal = jnp.tril(jnp.ones((C, C), dtype=bool))
    diff = b[:, :, :, :, None, :] - b[:, :, :, None, :, :]
    decay = jnp.exp(jnp.where(causal[:, :, None], diff, -jnp.inf))
    scores = jnp.einsum('nbhtd,nbhsd,nbhtsd->nbhts', qc, kc, decay)
    o_intra = jnp.einsum('nbhts,nbhsv->nbhtv', scores, vc)
    q_in = qc * jnp.exp(b)
    k_up = kc * jnp.exp(b_tot[:, :, :, None, :] - b)

    def step(state, xs):
        qi, ki, vi, bt = xs
        o = jnp.einsum('bhtd,bhdv->bhtv', qi, state)
        state = state * jnp.exp(bt)[..., None] + jnp.einsum('bhsd,bhsv->bhdv', ki, vi)
        return state, o

    s_fin, o_inter = lax.scan(step, s0.astype(jnp.float32), (q_in, k_up, vc, b_tot))
    o = (o_intra + o_inter).transpose(1, 0, 3, 2, 4).reshape(B, S, H, DV)
    return o, s_fin


def pool_mixer(p, pool_w, pool_scale):
    B, S, _ = p.shape
    pf = p.astype(jnp.float32)
    cs = jnp.concatenate([jnp.zeros((B, 1, POOL_DIM), jnp.float32), jnp.cumsum(pf, axis=1)], axis=1)
    pos = jnp.arange(S)
    diffs = []
    for gi, w in enumerate(POOL_WINDOWS):
        lo = jnp.clip(pos - w // 2, 0, S)
        hi = jnp.clip(pos + w // 2, 0, S)
        sl = slice(gi * POOL_GROUP_DIM, (gi + 1) * POOL_GROUP_DIM)
        cg = cs[:, :, sl]
        mean = (cg[:, hi] - cg[:, lo]) / (hi - lo).astype(jnp.float32)[None, :, None]
        diffs.append(mean - pf[:, :, sl])
    d = jnp.stack(diffs, axis=2)
    y = jnp.einsum('bsgi,gio->bsgo', d, pool_w.astype(jnp.float32)).reshape(B, S, POOL_DIM)
    return (y * pool_scale.astype(jnp.float32)).astype(p.dtype)


def mixer_ab(h, w_in, w_af, b_af, w_ab, b_ab, gla_norm, pool_w, pool_scale, w_out, s0_f, s0_b):
    B, S, _ = h.shape
    q, k, v, r, za_f, za_b, pin = split_cols(
        h @ w_in, (GLA_QK, GLA_QK, GLA_V, GLA_V, GLA_LOWRANK, GLA_LOWRANK, POOL_DIM))
    q = q.reshape(B, S, GLA_HEADS, GLA_DK) * (GLA_DK ** -0.5)
    k = k.reshape(B, S, GLA_HEADS, GLA_DK)
    v = v.reshape(B, S, GLA_HEADS, GLA_DV)
    la_f = (jax.nn.log_sigmoid((za_f @ w_af + b_af).astype(jnp.float32)) / GLA_TAU).reshape(B, S, GLA_HEADS, GLA_DK)
    la_b = (jax.nn.log_sigmoid((za_b @ w_ab + b_ab).astype(jnp.float32)) / GLA_TAU).reshape(B, S, GLA_HEADS, GLA_DK)
    o_f, s_f = gla_chunked(q, k, v, la_f, s0_f)
    o_b, s_b = gla_chunked(q[:, ::-1], k[:, ::-1], v[:, ::-1], la_b[:, ::-1], s0_b)
    o = rms_norm(o_f + o_b[:, ::-1], gla_norm).astype(h.dtype)
    gla_out = (o * jax.nn.silu(r.reshape(B, S, GLA_HEADS, GLA_DV))).reshape(B, S, GLA_V)
    pool_out = pool_mixer(pin, pool_w, pool_scale)
    y = jnp.concatenate([gla_out, pool_out], axis=-1) @ w_out
    return y, s_f, s_b


def attn_qkv(h, w_in, q_gain, k_gain):
    B, S, _ = h.shape
    q, k, v = split_cols(h @ w_in, (ATT_HEADS * ATT_HD, ATT_KV_HEADS * ATT_HD, ATT_KV_HEADS * ATT_HD))
    q = rms_norm(q.reshape(B, S, ATT_HEADS, ATT_HD), q_gain)
    k = rms_norm(k.reshape(B, S, ATT_KV_HEADS, ATT_HD), k_gain)
    v = v.reshape(B, S, ATT_KV_HEADS, ATT_HD)
    return q, k, v


def axial_rope(x):
    S = x.shape[1]
    rows = S // GRID_W
    row = jnp.repeat(jnp.arange(rows), GRID_W)
    col = jnp.tile(jnp.arange(GRID_W), rows)
    half = ATT_HD // 2
    inv_freq = 1.0 / (ROPE_THETA ** (jnp.arange(0, half, 2, dtype=jnp.float32) / half))
    xf = x.astype(jnp.float32)

    def rot(xa, pos):
        ang = pos.astype(jnp.float32)[:, None] * inv_freq[None, :]
        cos = jnp.cos(ang)[None, :, None, :]
        sin = jnp.sin(ang)[None, :, None, :]
        x1, x2 = xa[..., :half // 2], xa[..., half // 2:]
        return jnp.concatenate([x1 * cos - x2 * sin, x2 * cos + x1 * sin], axis=-1)

    return jnp.concatenate([rot(xf[..., :half], row), rot(xf[..., half:], col)], axis=-1).astype(x.dtype)


def context_attention(q, k, v, sink):
    B, L, H, HD = q.shape
    KV = k.shape[2]
    G = H // KV
    qg = q.reshape(B, L, KV, G, HD) * (HD ** -0.5)
    s = jnp.einsum('bqkgd,blkd->bkgql', qg, k).astype(jnp.float32)
    sk = jnp.broadcast_to(sink.astype(jnp.float32).reshape(KV, G)[None, :, :, None, None], s.shape[:-1] + (1,))
    probs = jax.nn.softmax(jnp.concatenate([s, sk], axis=-1), axis=-1)
    o = jnp.einsum('bkgql,blkd->bqkgd', probs[..., :L].astype(v.dtype), v)
    return o.reshape(B, L, H * HD)


def window_attention(q, k, v, k_ctx, v_ctx, sink):
    B, S, H, HD = q.shape
    KV = k.shape[2]
    G = H // KV
    W = ATT_BLOCK
    nb = S // W
    L = k_ctx.shape[1]
    qb = q.reshape(B, nb, W, KV, G, HD) * (HD ** -0.5)
    pad = ((0, 0), (W, W), (0, 0), (0, 0))
    kp = jnp.pad(k, pad).reshape(B, nb + 2, W, KV, HD)
    vp = jnp.pad(v, pad).reshape(B, nb + 2, W, KV, HD)
    kb = jnp.concatenate([kp[:, :-2], kp[:, 1:-1], kp[:, 2:]], axis=2)
    vb = jnp.concatenate([vp[:, :-2], vp[:, 1:-1], vp[:, 2:]], axis=2)
    s_loc = jnp.einsum('bnqkgd,bnjkd->bnkgqj', qb, kb).astype(jnp.float32)
    qpos = jnp.arange(S).reshape(nb, W)
    kpos = (jnp.arange(nb)[:, None] - 1) * W + jnp.arange(3 * W)[None, :]
    valid = ((kpos[:, None, :] >= 0) & (kpos[:, None, :] < S)
             & (jnp.abs(qpos[:, :, None] - kpos[:, None, :]) <= ATT_WINDOW))
    s_loc = jnp.where(valid[None, :, None, None], s_loc, -jnp.inf)
    s_ctx = jnp.einsum('bnqkgd,blkd->bnkgql', qb, k_ctx.astype(q.dtype)).astype(jnp.float32)
    sk = jnp.broadcast_to(sink.astype(jnp.float32).reshape(KV, G)[None, None, :, :, None, None],
                          s_loc.shape[:-1] + (1,))
    probs = jax.nn.softmax(jnp.concatenate([s_loc, s_ctx, sk], axis=-1), axis=-1)
    p_loc = probs[..., :3 * W].astype(v.dtype)
    p_ctx = probs[..., 3 * W:3 * W + L].astype(v.dtype)
    o = (jnp.einsum('bnkgqj,bnjkd->bnqkgd', p_loc, vb)
         + jnp.einsum('bnkgql,blkd->bnqkgd', p_ctx, v_ctx.astype(v.dtype)))
    return o.reshape(B, S, H * HD)


def peer(h, w_q, keys, u, v):
    B, S, D = h.shape
    T = B * S
    x = h.reshape(T, D)
    qh = (x @ w_q).reshape(T, PEER_HEADS, 2, PEER_HALF)
    s = jnp.einsum('thcd,hcnd->thcn', qh, keys).astype(jnp.float32)
    s1, i1 = lax.top_k(s[:, :, 0], PEER_TOPK)
    s2, i2 = lax.top_k(s[:, :, 1], PEER_TOPK)
    cand = (s1[..., :, None] + s2[..., None, :]).reshape(T, PEER_HEADS, PEER_TOPK * PEER_TOPK)
    cidx = (i1[..., :, None] * PEER_NKEYS + i2[..., None, :]).reshape(T, PEER_HEADS, PEER_TOPK * PEER_TOPK)
    top_s, top_pos = lax.top_k(cand, PEER_TOPK)
    experts = jnp.take_along_axis(cidx, top_pos, axis=-1)
    gates = jax.nn.softmax(top_s, axis=-1)
    nblk = T // PEER_TOK_BLOCK

    def block(args):
        xb, eb, gb = args
        act = jax.nn.gelu(jnp.einsum('thkd,td->thk', u[eb], xb))
        return jnp.einsum('thk,thkd->td', (gb * act).astype(xb.dtype), v[eb])

    y = lax.map(block, (x.reshape(nblk, PEER_TOK_BLOCK, D),
                        experts.reshape(nblk, PEER_TOK_BLOCK, PEER_HEADS, PEER_TOPK),
                        gates.reshape(nblk, PEER_TOK_BLOCK, PEER_HEADS, PEER_TOPK)))
    return y.reshape(B, S, D)


def setup_inputs(seed: int = 0) -> dict:
    key = jax.random.key(seed)
    ks = iter(jax.random.split(key, 64))

    def nrm(shape, scale=1.0):
        return jax.random.normal(next(ks), shape, jnp.float32) * scale

    D = D_MODEL
    inp = {}
    inp['x_prompt'] = nrm((BATCH, SEQ, D))
    inp['x_sample'] = nrm((DEC_BATCH, DEC_SEQ, D))
    inp['state_l0_gla'] = nrm((DEC_BATCH, 2, GLA_HEADS, GLA_DK, GLA_DV))
    inp['cache_l1_k'] = nrm((DEC_BATCH, PAST_LEN, ATT_KV_HEADS, ATT_HD))
    inp['cache_l1_v'] = nrm((DEC_BATCH, PAST_LEN, ATT_KV_HEADS, ATT_HD))
    inp['c'] = nrm((DEC_BATCH, D))
    inp['c_ctx'] = nrm((D,))
    inp['l0_w_mod'] = nrm((D, 6 * D), 0.5 * D ** -0.5)
    inp['l0_b_mod'] = nrm((6 * D,), 0.05)
    inp['l0_norm1'] = 1.0 + nrm((D,), 0.05)
    inp['l0_w_in'] = nrm((D, IN0_DIM), D ** -0.5)
    inp['l0_w_alpha_f'] = nrm((GLA_LOWRANK, GLA_QK), GLA_LOWRANK ** -0.5)
    inp['l0_b_alpha_f'] = 2.0 + nrm((GLA_QK,), 0.5)
    inp['l0_w_alpha_b'] = nrm((GLA_LOWRANK, GLA_QK), GLA_LOWRANK ** -0.5)
    inp['l0_b_alpha_b'] = 2.0 + nrm((GLA_QK,), 0.5)
    inp['l0_gla_norm'] = 1.0 + nrm((GLA_DV,), 0.05)
    inp['l0_pool_w'] = nrm((POOL_GROUPS, POOL_GROUP_DIM, POOL_GROUP_DIM), POOL_GROUP_DIM ** -0.5)
    inp['l0_pool_scale'] = 0.5 + nrm((POOL_DIM,), 0.1)
    inp['l0_w_out'] = nrm((MIX0_DIM, D), MIX0_DIM ** -0.5)
    inp['l0_norm2'] = 1.0 + nrm((D,), 0.05)
    inp['l0_peer_wq'] = nrm((D, PEER_HEADS * PEER_QDIM), D ** -0.5)
    inp['l0_peer_keys'] = nrm((PEER_HEADS, 2, PEER_NKEYS, PEER_HALF), PEER_HALF ** -0.5)
    inp['l0_peer_u'] = nrm((PEER_N, D), D ** -0.5)
    inp['l0_peer_v'] = nrm((PEER_N, D), 0.5)
    inp['l1_w_mod'] = nrm((D, 6 * D), 0.5 * D ** -0.5)
    inp['l1_b_mod'] = nrm((6 * D,), 0.05)
    inp['l1_norm1'] = 1.0 + nrm((D,), 0.05)
    inp['l1_w_in'] = nrm((D, IN1_DIM), D ** -0.5)
    inp['l1_q_norm'] = 1.0 + nrm((ATT_HD,), 0.05)
    inp['l1_k_norm'] = 1.0 + nrm((ATT_HD,), 0.05)
    inp['l1_sink'] = nrm((ATT_HEADS,), 0.5)
    inp['l1_w_out'] = nrm((ATT_HEADS * ATT_HD, D), (ATT_HEADS * ATT_HD) ** -0.5)
    inp['l1_norm2'] = 1.0 + nrm((D,), 0.05)
    inp['l1_peer_wq'] = nrm((D, PEER_HEADS * PEER_QDIM), D ** -0.5)
    inp['l1_peer_keys'] = nrm((PEER_HEADS, 2, PEER_NKEYS, PEER_HALF), PEER_HALF ** -0.5)
    inp['l1_peer_u'] = nrm((PEER_N, D), D ** -0.5)
    inp['l1_peer_v'] = nrm((PEER_N, D), 0.5)
    return inp


def reference(x_prompt, x_sample, state_l0_gla, cache_l1_k, cache_l1_v, c, c_ctx,
              l0_w_mod, l0_b_mod, l0_norm1, l0_w_in, l0_w_alpha_f, l0_b_alpha_f, l0_w_alpha_b, l0_b_alpha_b,
              l0_gla_norm, l0_pool_w, l0_pool_scale, l0_w_out, l0_norm2, l0_peer_wq, l0_peer_keys, l0_peer_u, l0_peer_v,
              l1_w_mod, l1_b_mod, l1_norm1, l1_w_in, l1_q_norm, l1_k_norm, l1_sink, l1_w_out,
              l1_norm2, l1_peer_wq, l1_peer_keys, l1_peer_u, l1_peer_v):
    layer_mod = ((l0_w_mod, l0_b_mod, l0_norm1), (l1_w_mod, l1_b_mod, l1_norm1))
    layer_peer = ((l0_norm2, l0_peer_wq, l0_peer_keys, l0_peer_u, l0_peer_v),
                  (l1_norm2, l1_peer_wq, l1_peer_keys, l1_peer_u, l1_peer_v))
    mix_ab = (l0_w_in, l0_w_alpha_f, l0_b_alpha_f, l0_w_alpha_b, l0_b_alpha_b,
              l0_gla_norm, l0_pool_w, l0_pool_scale, l0_w_out)

    cond_ctx = c_ctx[None, :]
    xp = x_prompt
    bp = x_prompt.shape[0]
    for li in range(DEPTH):
        w_mod, b_mod, n1 = layer_mod[li]
        sh1, sc1, g1, sh2, sc2, g2 = modulation(cond_ctx, w_mod, b_mod)
        h = modulate(xp, n1, sh1, sc1)
        if li % 2 == 0:
            s0 = jnp.zeros((bp, GLA_HEADS, GLA_DK, GLA_DV), jnp.float32)
            out, s_f, s_b = mixer_ab(h, *mix_ab, s0, s0)
            new_state_l0_gla = jnp.stack([s_f, s_b], axis=1).astype(x_prompt.dtype)
        else:
            q, k, v = attn_qkv(h, l1_w_in, l1_q_norm, l1_k_norm)
            out = context_attention(q, k, v, l1_sink) @ l1_w_out
            new_cache_l1_k = k
            new_cache_l1_v = v
        xp = xp + g1 * out
        n2, wq, pkeys, pu, pv = layer_peer[li]
        xp = xp + g2 * peer(modulate(xp, n2, sh2, sc2), wq, pkeys, pu, pv)
    y_prompt = xp

    xs = x_sample
    for li in range(DEPTH):
        w_mod, b_mod, n1 = layer_mod[li]
        sh1, sc1, g1, sh2, sc2, g2 = modulation(c, w_mod, b_mod)
        h = modulate(xs, n1, sh1, sc1)
        if li % 2 == 0:
            out, _, _ = mixer_ab(h, *mix_ab, state_l0_gla[:, 0], state_l0_gla[:, 1])
        else:
            q, k, v = attn_qkv(h, l1_w_in, l1_q_norm, l1_k_norm)
            q = axial_rope(q)
            k = axial_rope(k)
            out = window_attention(q, k, v, cache_l1_k, cache_l1_v, l1_sink) @ l1_w_out
        xs = xs + g1 * out
        n2, wq, pkeys, pu, pv = layer_peer[li]
        xs = xs + g2 * peer(modulate(xs, n2, sh2, sc2), wq, pkeys, pu, pv)
    y_sample = xs

    return (y_prompt, y_sample, new_state_l0_gla, new_cache_l1_k, new_cache_l1_v)
```

```python
import functools

import numpy as np
import jax
import jax.numpy as jnp
from jax import lax
from jax.experimental import pallas as pl
from jax.experimental.pallas import tpu as pltpu

F32 = jnp.float32
BF16 = jnp.bfloat16

D_MODEL = 1024
EPS = 1e-6
LANES = 128
MOD_GROUP = 1024
VMEM_LIMIT = 48 * 1024 * 1024

GLA_HEADS = 4
GLA_DK = 64
GLA_DV = 128
GLA_TAU = 16.0
GLA_CHUNK = 64
POOL_WINDOWS = (2, 4, 8, 16)
POOL_GROUP_DIM = 128

ATT_HEADS = 16
ATT_KV_HEADS = 4
ATT_HD = 64
ATT_GROUP = ATT_HEADS // ATT_KV_HEADS
ATT_WINDOW = 128
GRID_W = 64
ROPE_THETA = 10000.0

PEER_HEADS = 8
PEER_NKEYS = 128
PEER_TOPK = 16
PEER_HALF = 128
NEG = -3.0e38


def _dot(a, b):
    return jnp.dot(a, b, preferred_element_type=F32)


def _dot_nt(a, b):
    return lax.dot_general(a, b, (((1,), (1,)), ((), ())), preferred_element_type=F32)


def _dot_tn(a, b):
    return lax.dot_general(a, b, (((0,), (0,)), ((), ())), preferred_element_type=F32)


def _dot_hi(a, b):
    return jnp.dot(a, b, precision=lax.Precision.HIGHEST, preferred_element_type=F32)


def _silu(x):
    return x / (1.0 + jnp.exp(-x))


def _log_sigmoid(x):
    return jnp.minimum(x, 0.0) - jnp.log(1.0 + jnp.exp(-jnp.abs(x)))


def _params(*sem):
    return pltpu.CompilerParams(dimension_semantics=sem, vmem_limit_bytes=VMEM_LIMIT)


def _mod_kernel(cond_ref, w_ref, b_ref, o_ref):
    o_ref[...] = _dot_hi(_silu(cond_ref[...]), w_ref[...]) + b_ref[...]


def _modulation(cond8, w_mod, b_mod):
    n = w_mod.shape[1]
    tn = 1024
    out = pl.pallas_call(
        _mod_kernel,
        grid=(n // tn,),
        in_specs=[pl.BlockSpec((8, D_MODEL), lambda j: (0, 0)),
                  pl.BlockSpec((D_MODEL, tn), lambda j: (0, j)),
                  pl.BlockSpec((1, tn), lambda j: (0, j))],
        out_specs=pl.BlockSpec((8, tn), lambda j: (0, j)),
        out_shape=jax.ShapeDtypeStruct((8, n), F32),
        compiler_params=_params("arbitrary"),
        name="modulation",
    )(cond8, w_mod, b_mod.reshape(1, n))
    return out.reshape(8, 6, D_MODEL)


def _modulated(x, gain, mod_ref, shift_idx):
    ms = jnp.mean(x * x, axis=-1, keepdims=True)
    y = x * lax.rsqrt(ms + EPS) * gain
    return y * (1.0 + mod_ref[0, shift_idx + 1:shift_idx + 2, :]) + mod_ref[0, shift_idx:shift_idx + 1, :]


def _proj_kernel(x_ref, mod_ref, gain_ref, w_ref, *o_refs, splits):
    hb = _modulated(x_ref[...], gain_ref[...], mod_ref, 0).astype(BF16)
    off = 0
    for o_ref, n in zip(o_refs, splits):
        o_ref[...] = _dot(hb, w_ref[:, off:off + n])
        off += n


def _in_proj(x, mod, mod_row, gain, w_bf16, splits, tm=512):
    t = x.shape[0]
    ntot = w_bf16.shape[1]
    return pl.pallas_call(
        functools.partial(_proj_kernel, splits=splits),
        grid=(t // tm,),
        in_specs=[pl.BlockSpec((tm, D_MODEL), lambda i: (i, 0)),
                  pl.BlockSpec((1, 6, D_MODEL), lambda i: (mod_row(i * tm), 0, 0)),
                  pl.BlockSpec((1, D_MODEL), lambda i: (0, 0)),
                  pl.BlockSpec((D_MODEL, ntot), lambda i: (0, 0))],
        out_specs=[pl.BlockSpec((tm, n), lambda i: (i, 0)) for n in splits],
        out_shape=[jax.ShapeDtypeStruct((t, n), F32) for n in splits],
        compiler_params=_params("arbitrary"),
        name="in_proj",
    )(x, mod, gain.reshape(1, D_MODEL), w_bf16)


def _gla_direction(q, k, v, la, state_t, o_ref, *, seq, reverse, accumulate):
    c = GLA_CHUNK
    n = seq // c
    row = lax.broadcasted_iota(jnp.int32, (c, c), 0)
    col = lax.broadcasted_iota(jnp.int32, (c, c), 1)
    keep = (row <= col) if reverse else (row >= col)
    tri = keep.astype(F32)
    order = range(n - 1, -1, -1) if reverse else range(n)
    last = 0 if reverse else c - 1
    for ci in order:
        sl = slice(ci * c, (ci + 1) * c)
        qc, kc, vc, lac = q[sl], k[sl], v[sl], la[sl]
        b = _dot_hi(tri, lac)
        btot = b[last:last + 1]
        bmid = b[c // 2 - 1:c // 2]
        qm = (qc * jnp.exp(b - bmid)).astype(BF16)
        km = (kc * jnp.exp(bmid - b)).astype(BF16)
        scores = jnp.where(keep, _dot_nt(qm, km), 0.0)
        o = _dot(scores.astype(BF16), vc.astype(BF16))
        q_in = (qc * jnp.exp(b)).astype(BF16)
        k_up = (kc * jnp.exp(btot - b)).astype(BF16)
        o = o + _dot_nt(q_in, state_t.astype(BF16))
        state_t = state_t * jnp.exp(btot) + _dot_tn(vc.astype(BF16), k_up)
        if accumulate:
            o_ref[sl, :] = o_ref[sl, :] + o
        else:
            o_ref[sl, :] = o
    return state_t


def _gla_kernel(q_ref, k_ref, v_ref, r_ref, za_ref, s0_ref, waf_ref, baf_ref, wab_ref, bab_ref, gn_ref,
                go_ref, sfin_ref, o_scr, *, seq):
    q = q_ref[...] * (GLA_DK ** -0.5)
    k = k_ref[...]
    v = v_ref[...]
    zab = za_ref[...].astype(BF16)
    la_f = _log_sigmoid(_dot(zab, waf_ref[...]) + baf_ref[...]) * (1.0 / GLA_TAU)
    la_b = _log_sigmoid(_dot(zab, wab_ref[...]) + bab_ref[...]) * (1.0 / GLA_TAU)
    pad = jnp.zeros((LANES - GLA_DK, GLA_DV), F32)
    for d, la in ((0, la_f), (1, la_b)):
        s0_t = jnp.concatenate([s0_ref[0, d, 0], pad], axis=0).T
        s_t = _gla_direction(q, k, v, la, s0_t, o_scr, seq=seq, reverse=(d == 1), accumulate=(d == 1))
        sfin_ref[0, d, 0] = s_t.T[:GLA_DK, :]
    o = o_scr[...]
    o = o * lax.rsqrt(jnp.mean(o * o, axis=-1, keepdims=True) + EPS) * gn_ref[...]
    go_ref[...] = o * _silu(r_ref[...])


def _gla(qp, kp, v, r, za, s0, waf, baf, wab, bab, gla_norm, *, seq, row0):
    nseq = s0.shape[0]
    head_blk = lambda b, h: (row0 + b, h)
    st_blk = pl.BlockSpec((1, 2, 1, GLA_DK, GLA_DV), lambda b, h: (b, 0, h, 0, 0))
    w_blk = pl.BlockSpec((LANES, LANES), lambda b, h: (0, h))
    b_blk = pl.BlockSpec((1, LANES), lambda b, h: (0, h))
    return pl.pallas_call(
        functools.partial(_gla_kernel, seq=seq),
        grid=(nseq, GLA_HEADS),
        in_specs=[pl.BlockSpec((seq, LANES), head_blk)] * 4
        + [pl.BlockSpec((seq, LANES), lambda b, h: (row0 + b, 0)), st_blk, w_blk, b_blk, w_blk, b_blk,
           pl.BlockSpec((1, LANES), lambda b, h: (0, 0))],
        out_specs=[pl.BlockSpec((seq, LANES), lambda b, h: (b, h)), st_blk],
        out_shape=[jax.ShapeDtypeStruct((nseq * seq, GLA_HEADS * GLA_DV), F32),
                   jax.ShapeDtypeStruct((nseq, 2, GLA_HEADS, GLA_DK, GLA_DV), F32)],
        scratch_shapes=[pltpu.VMEM((seq, LANES), F32)],
        compiler_params=_params("arbitrary", "arbitrary"),
        name="gla",
    )(qp, kp, v, r, za, s0, waf, baf, wab, bab, gla_norm)


def _mix0_out_kernel(go_ref, pin_ref, x_ref, mod_ref, pw_ref, ps_ref, wo_ref, o_ref, *, seq, tr):
    t0 = pl.multiple_of(pl.program_id(1) * tr, tr)
    rows = t0 + lax.broadcasted_iota(jnp.int32, (tr, seq), 0)
    cols = lax.broadcasted_iota(jnp.int32, (tr, seq), 1)
    rcol = t0 + lax.broadcasted_iota(jnp.int32, (tr, 1), 0)
    y = _dot(go_ref[...].astype(BF16), wo_ref[0:GLA_HEADS * GLA_DV, :])
    for g, w in enumerate(POOL_WINDOWS):
        sl = slice(g * POOL_GROUP_DIM, (g + 1) * POOL_GROUP_DIM)
        lo = jnp.maximum(rows - w // 2, 0)
        hi = jnp.minimum(rows + w // 2, seq)
        band = jnp.where((cols >= lo) & (cols < hi), 1.0, 0.0).astype(BF16)
        pg = pin_ref[:, sl]
        pg_hi = pg.astype(BF16)
        pg_lo = (pg - pg_hi.astype(F32)).astype(BF16)
        wsum = _dot(band, pg_hi) + _dot(band, pg_lo)
        cnt = (jnp.minimum(rcol + w // 2, seq) - jnp.maximum(rcol - w // 2, 0)).astype(F32)
        dg = wsum / cnt - pin_ref[pl.ds(t0, tr), sl]
        yg = _dot(dg.astype(BF16), pw_ref[g]) * ps_ref[:, sl]
        y = y + _dot(yg.astype(BF16), wo_ref[GLA_HEADS * GLA_DV + g * POOL_GROUP_DIM:
                                             GLA_HEADS * GLA_DV + (g + 1) * POOL_GROUP_DIM, :])
    o_ref[...] = x_ref[...] + mod_ref[0, 2:3, :] * y


def _mix0_out(go, pin, x, mod, mod_row, pool_w, pool_scale, w_out, *, seq, tr=256):
    t = x.shape[0]
    nseq, nr = t // seq, seq // tr
    return pl.pallas_call(
        functools.partial(_mix0_out_kernel, seq=seq, tr=tr),
        grid=(nseq, nr),
        in_specs=[pl.BlockSpec((tr, 512), lambda b, r: (b * nr + r, 0)),
                  pl.BlockSpec((seq, 512), lambda b, r: (b, 0)),
                  pl.BlockSpec((tr, D_MODEL), lambda b, r: (b * nr + r, 0)),
                  pl.BlockSpec((1, 6, D_MODEL), lambda b, r: (mod_row(b * seq), 0, 0)),
                  pl.BlockSpec((4, POOL_GROUP_DIM, POOL_GROUP_DIM), lambda b, r: (0, 0, 0)),
                  pl.BlockSpec((1, 512), lambda b, r: (0, 0)),
                  pl.BlockSpec((D_MODEL, D_MODEL), lambda b, r: (0, 0))],
        out_specs=pl.BlockSpec((tr, D_MODEL), lambda b, r: (b * nr + r, 0)),
        out_shape=jax.ShapeDtypeStruct((t, D_MODEL), F32),
        compiler_params=_params("arbitrary", "arbitrary"),
        name="mix0_out",
    )(go, pin, x, mod, pool_w, pool_scale, w_out)


def _head_norm(x, gain):
    ms = jnp.sum(x * x, axis=-1, keepdims=True) * (1.0 / ATT_HD)
    return x * lax.rsqrt(ms + EPS) * gain


def _rope(x, cos, sin_signed):
    lane = lax.broadcasted_iota(jnp.int32, x.shape, 1)
    partner = jnp.where((lane & 31) < 16, pltpu.roll(x, LANES - 16, 1), pltpu.roll(x, 16, 1))
    return x * cos + partner * sin_signed


def _ctx_attn_kernel(q_ref, k_ref, v_ref, x_ref, mod_ref, qg_ref, kg_ref, sink_ref, wo_ref,
                     o_ref, ko_ref):
    y = jnp.zeros(o_ref.shape, F32)
    for kv in range(ATT_KV_HEADS):
        ksl = slice(kv * LANES, (kv + 1) * LANES)
        k = _head_norm(k_ref[:, ksl], kg_ref[...])
        ko_ref[:, ksl] = k
        kb = k.astype(BF16)
        vb = v_ref[:, ksl].astype(BF16)
        for g in range(ATT_GROUP):
            h = kv * ATT_GROUP + g
            hsl = slice(h * LANES, (h + 1) * LANES)
            q = _head_norm(q_ref[:, hsl], qg_ref[...]) * (ATT_HD ** -0.5)
            s = _dot_nt(q.astype(BF16), kb)
            sink = sink_ref[h]
            m = jnp.maximum(jnp.max(s, axis=-1, keepdims=True), sink)
            p = jnp.exp(s - m)
            denom = jnp.sum(p, axis=-1, keepdims=True) + jnp.exp(sink - m)
            o = _dot(p.astype(BF16), vb) / denom
            y = y + _dot(o.astype(BF16), wo_ref[hsl, :])
    o_ref[...] = x_ref[...] + mod_ref[0, 2:3, :] * y


def _ctx_attention(q, k, v, x, mod, q_gain, k_gain, sink, w_out_p, *, seq):
    t = x.shape[0]
    nseq = t // seq
    row = lambda b: (b, 0)
    return pl.pallas_call(
        _ctx_attn_kernel,
        grid=(nseq,),
        in_specs=[pl.BlockSpec((seq, ATT_HEADS * LANES), row),
                  pl.BlockSpec((seq, ATT_KV_HEADS * LANES), row),
                  pl.BlockSpec((seq, ATT_KV_HEADS * LANES), row),
                  pl.BlockSpec((seq, D_MODEL), row),
                  pl.BlockSpec((1, 6, D_MODEL), lambda b: (0, 0, 0)),
                  pl.BlockSpec((1, LANES), lambda b: (0, 0)),
                  pl.BlockSpec((1, LANES), lambda b: (0, 0)),
                  pl.BlockSpec(memory_space=pltpu.SMEM),
                  pl.BlockSpec((ATT_HEADS * LANES, D_MODEL), lambda b: (0, 0))],
        out_specs=[pl.BlockSpec((seq, D_MODEL), row),
                   pl.BlockSpec((seq, ATT_KV_HEADS * LANES), row)],
        out_shape=[jax.ShapeDtypeStruct((t, D_MODEL), F32),
                   jax.ShapeDtypeStruct((t, ATT_KV_HEADS * LANES), F32)],
        compiler_params=_params("arbitrary"),
        name="ctx_attention",
    )(q, k, v, x, mod, q_gain, k_gain, sink, w_out_p)


def _win_attn_kernel(q_ref, k_ref, v_ref, kc_ref, vc_ref, cq_ref, sq_ref, ck_ref, sk_ref, x_ref, mod_ref,
                     qg_ref, kg_ref, sink_ref, wo_ref, o_ref, kr_scr, *, seq, tq):
    i = pl.program_id(1)
    nb = seq // tq

    @pl.when(i == 0)
    def _():
        for kv in range(ATT_KV_HEADS):
            ksl = slice(kv * LANES, (kv + 1) * LANES)
            kr_scr[:, ksl] = _rope(_head_norm(k_ref[:, ksl], kg_ref[...]), ck_ref[...], sk_ref[...])

    row = lax.broadcasted_iota(jnp.int32, (tq, tq), 0)
    col = lax.broadcasted_iota(jnp.int32, (tq, tq), 1)
    band = {-1: col >= row, 0: None, 1: col <= row}
    y = jnp.zeros(o_ref.shape, F32)
    for kv in range(ATT_KV_HEADS):
        ksl = slice(kv * LANES, (kv + 1) * LANES)
        kcb = kc_ref[:, ksl].astype(BF16)
        vcb = vc_ref[:, ksl].astype(BF16)
        kloc, vloc = {}, {}
        for j in (-1, 0, 1):
            start = pl.multiple_of(jnp.clip(i + j, 0, nb - 1) * tq, tq)
            kloc[j] = kr_scr[pl.ds(start, tq), ksl].astype(BF16)
            vloc[j] = v_ref[pl.ds(start, tq), ksl].astype(BF16)
        for g in range(ATT_GROUP):
            h = kv * ATT_GROUP + g
            hsl = slice(h * LANES, (h + 1) * LANES)
            q = _rope(_head_norm(q_ref[:, hsl], qg_ref[...]), cq_ref[...], sq_ref[...]) * (ATT_HD ** -0.5)
            qb = q.astype(BF16)
            sink = sink_ref[h]
            s_loc = {}
            for j in (-1, 0, 1):
                s = _dot_nt(qb, kloc[j])
                if j != 0:
                    limit = jnp.where((i + j >= 0) & (i + j < nb), -NEG, NEG)
                    s = jnp.minimum(jnp.where(band[j], s, NEG), limit)
                s_loc[j] = s
            s_ctx = _dot_nt(qb, kcb)
            m = jnp.maximum(jnp.max(s_ctx, axis=-1, keepdims=True), sink)
            for j in (-1, 0, 1):
                m = jnp.maximum(m, jnp.max(s_loc[j], axis=-1, keepdims=True))
            p_ctx = jnp.exp(s_ctx - m)
            denom = jnp.sum(p_ctx, axis=-1, keepdims=True) + jnp.exp(sink - m)
            o = _dot(p_ctx.astype(BF16), vcb)
            for j in (-1, 0, 1):
                p = jnp.exp(s_loc[j] - m)
                denom = denom + jnp.sum(p, axis=-1, keepdims=True)
                o = o + _dot(p.astype(BF16), vloc[j])
            y = y + _dot((o / denom).astype(BF16), wo_ref[hsl, :])
    o_ref[...] = x_ref[...] + mod_ref[0, 2:3, :] * y


def _win_attention(q, k, v, kc, vc, cos, sin_signed, x, mod, q_gain, k_gain, sink, w_out_p, *, seq, tq=128):
    t = x.shape[0]
    nseq, nb = t // seq, seq // tq
    ctx = kc.shape[0] // nseq
    qrow = lambda b, i: (b * nb + i, 0)
    srow = lambda b, i: (b, 0)
    one = lambda b, i: (0, 0)
    return pl.pallas_call(
        functools.partial(_win_attn_kernel, seq=seq, tq=tq),
        grid=(nseq, nb),
        in_specs=[pl.BlockSpec((tq, ATT_HEADS * LANES), qrow),
                  pl.BlockSpec((seq, ATT_KV_HEADS * LANES), srow),
                  pl.BlockSpec((seq, ATT_KV_HEADS * LANES), srow),
                  pl.BlockSpec((ctx, ATT_KV_HEADS * LANES), srow),
                  pl.BlockSpec((ctx, ATT_KV_HEADS * LANES), srow),
                  pl.BlockSpec((tq, LANES), lambda b, i: (i, 0)),
                  pl.BlockSpec((tq, LANES), lambda b, i: (i, 0)),
                  pl.BlockSpec((seq, LANES), one),
                  pl.BlockSpec((seq, LANES), one),
                  pl.BlockSpec((tq, D_MODEL), qrow),
                  pl.BlockSpec((1, 6, D_MODEL), lambda b, i: (4 + b, 0, 0)),
                  pl.BlockSpec((1, LANES), one),
                  pl.BlockSpec((1, LANES), one),
                  pl.BlockSpec(memory_space=pltpu.SMEM),
                  pl.BlockSpec((ATT_HEADS * LANES, D_MODEL), one)],
        out_specs=pl.BlockSpec((tq, D_MODEL), qrow),
        out_shape=jax.ShapeDtypeStruct((t, D_MODEL), F32),
        scratch_shapes=[pltpu.VMEM((seq, ATT_KV_HEADS * LANES), F32)],
        compiler_params=_params("arbitrary", "arbitrary"),
        name="win_attention",
    )(q, k, v, kc, vc, cos, sin_signed, cos, sin_signed, x, mod, q_gain, k_gain, sink, w_out_p)


def _top16(s, vals_ref, slot):
    key = lax.broadcasted_iota(jnp.int32, s.shape, 0).astype(F32)
    work = s
    rank = jnp.full(s.shape, float(PEER_NKEYS - 1), F32)
    for kk in range(PEER_TOPK):
        m = jnp.max(work, axis=0, keepdims=True)
        first = jnp.min(jnp.where(work == m, key, float(PEER_NKEYS)), axis=0, keepdims=True)
        sel = key == first
        rank = jnp.where(sel, float(kk), rank)
        work = jnp.where(sel, NEG, work)
        vals_ref[slot, kk:kk + 1, :] = m
    return rank


def _route_kernel(x_ref, mod_ref, gain_ref, wq_ref, keys_ref, ht_ref, rank2_ref, e2_ref, n_ref, e1_ref,
                  hb_scr, vals_scr):
    tm = x_ref.shape[0]

    @pl.when(pl.program_id(1) == 0)
    def _():
        h = _modulated(x_ref[...], gain_ref[...], mod_ref, 3)
        hb_scr[...] = h.astype(BF16)
        ht_ref[...] = h.T.astype(BF16)

    q = _dot(hb_scr[...], wq_ref[...])
    s1 = _dot_nt(keys_ref[0, 0].astype(BF16), q[:, :PEER_HALF].astype(BF16))
    s2 = _dot_nt(keys_ref[0, 1].astype(BF16), q[:, PEER_HALF:].astype(BF16))
    rank1 = _top16(s1, vals_scr, 0)
    rank2 = _top16(s2, vals_scr, 1)
    a = vals_scr[0]
    b = vals_scr[1]
    blocks = [a[0:1] + b]
    blocks += [a[p:p + 1] + b[0:8] for p in range(1, 8)]
    blocks += [a[8:16] + b[0:1]]
    cand = jnp.concatenate(blocks, axis=0)
    r = lax.broadcasted_iota(jnp.int32, cand.shape, 0)
    p_idx = jnp.where(r < 16, 0, jnp.where(r < 72, ((r - 16) >> 3) + 1, r - 64))
    q_idx = jnp.where(r < 16, r, jnp.where(r < 72, (r - 16) & 7, 0))
    flat = (p_idx * PEER_TOPK + q_idx).astype(F32)
    work = jnp.where((p_idx + 1) * (q_idx + 1) <= PEER_TOPK, cand, NEG)
    chosen = jnp.zeros(cand.shape, F32)
    for _ in range(PEER_TOPK):
        m = jnp.max(work, axis=0, keepdims=True)
        first = jnp.min(jnp.where(work == m, flat, float(PEER_TOPK * PEER_TOPK)), axis=0, keepdims=True)
        sel = flat == first
        chosen = jnp.where(sel, 1.0, chosen)
        work = jnp.where(sel, NEG, work)
    z = jnp.sum(chosen * jnp.exp(cand - cand[0:1]), axis=0, keepdims=True)
    counts = [jnp.sum(chosen[0:16], axis=0, keepdims=True)]
    counts += [jnp.sum(chosen[8 + 8 * p:16 + 8 * p], axis=0, keepdims=True) for p in range(1, 8)]
    counts += [chosen[64 + p:65 + p] for p in range(8, 16)]
    n_sel = jnp.zeros(s1.shape, F32)
    for p in range(PEER_TOPK):
        n_sel = jnp.where(rank1 == float(p), counts[p], n_sel)
    rank2_ref[0] = rank2
    n_ref[0] = n_sel
    e1_ref[0] = jnp.exp(s1 - a[0:1]) / z
    e2_ref[0] = jnp.exp(s2 - b[0:1])


def _peer_route(x, mod, mod_row, gain, wq_bf16, keys, tm=512):
    t = x.shape[0]
    route = jax.ShapeDtypeStruct((PEER_HEADS, PEER_NKEYS, t), F32)
    rblk = pl.BlockSpec((1, PEER_NKEYS, tm), lambda i, h: (h, 0, i))
    return pl.pallas_call(
        _route_kernel,
        grid=(t // tm, PEER_HEADS),
        in_specs=[pl.BlockSpec((tm, D_MODEL), lambda i, h: (i, 0)),
                  pl.BlockSpec((1, 6, D_MODEL), lambda i, h: (mod_row(i * tm), 0, 0)),
                  pl.BlockSpec((1, D_MODEL), lambda i, h: (0, 0)),
                  pl.BlockSpec((D_MODEL, 2 * PEER_HALF), lambda i, h: (0, h)),
                  pl.BlockSpec((1, 2, PEER_NKEYS, PEER_HALF), lambda i, h: (h, 0, 0, 0))],
        out_specs=[pl.BlockSpec((D_MODEL, tm), lambda i, h: (0, i)), rblk, rblk, rblk, rblk],
        out_shape=[jax.ShapeDtypeStruct((D_MODEL, t), BF16), route, route, route, route],
        scratch_shapes=[pltpu.VMEM((tm, D_MODEL), BF16), pltpu.VMEM((2, PEER_TOPK, tm), F32)],
        compiler_params=_params("arbitrary", "arbitrary"),
        name="peer_route",
    )(x, mod, gain.reshape(1, D_MODEL), wq_bf16, keys)


def _gelu_tanh(x):
    return 0.5 * x * (1.0 + jnp.tanh(0.7978845608028654 * (x + 0.044715 * (x * x * x))))


def _expert_kernel(ht_ref, u_ref, vt_ref, rank2_ref, e2_ref, n_ref, e1_ref, x_ref, mod_ref, o_ref, acc_scr,
                   *, ic):
    c = pl.program_id(1)

    @pl.when(c == 0)
    def _():
        acc_scr[...] = jnp.zeros(acc_scr.shape, F32)

    a_t = _dot(u_ref[...].astype(BF16), ht_ref[...])
    parts = []
    for ii in range(ic):
        i = c * ic + ii
        gate = jnp.zeros((PEER_NKEYS, a_t.shape[1]), F32)
        for h in range(PEER_HEADS):
            n_row = n_ref[h, pl.ds(i, 1), :]
            e1_row = e1_ref[h, pl.ds(i, 1), :]
            gate = gate + jnp.where(rank2_ref[h] < n_row, e2_ref[h], 0.0) * e1_row
        parts.append((gate * _gelu_tanh(a_t[ii * PEER_NKEYS:(ii + 1) * PEER_NKEYS])).astype(BF16))
    w_t = jnp.concatenate(parts, axis=0)
    acc_scr[...] += _dot(vt_ref[...], w_t)

    @pl.when(c == pl.num_programs(1) - 1)
    def _():
        o_ref[...] = x_ref[...] + mod_ref[0, 5:6, :] * acc_scr[...].T


def _peer_experts(ht, u, vt_bf16, rank2, e2, n_sel, e1, x, mod, mod_row, tt=512, ic=4):
    t = x.shape[0]
    ec = ic * PEER_NKEYS
    rblk = pl.BlockSpec((PEER_HEADS, PEER_NKEYS, tt), lambda i, c: (0, 0, i))
    return pl.pallas_call(
        functools.partial(_expert_kernel, ic=ic),
        grid=(t // tt, PEER_NKEYS // ic),
        in_specs=[pl.BlockSpec((D_MODEL, tt), lambda i, c: (0, i)),
                  pl.BlockSpec((ec, D_MODEL), lambda i, c: (c, 0)),
                  pl.BlockSpec((D_MODEL, ec), lambda i, c: (0, c)),
                  rblk, rblk, rblk, rblk,
                  pl.BlockSpec((tt, D_MODEL), lambda i, c: (i, 0)),
                  pl.BlockSpec((1, 6, D_MODEL), lambda i, c: (mod_row(i * tt), 0, 0))],
        out_specs=pl.BlockSpec((tt, D_MODEL), lambda i, c: (i, 0)),
        out_shape=jax.ShapeDtypeStruct((t, D_MODEL), F32),
        scratch_shapes=[pltpu.VMEM((D_MODEL, tt), F32)],
        compiler_params=_params("arbitrary", "arbitrary"),
        name="peer_experts",
    )(ht, u, vt_bf16, rank2, e2, n_sel, e1, x, mod)


def _peer(x, mod, mod_row, gain, wq_bf16, keys, u, vt_bf16):
    ht, rank2, e2, n_sel, e1 = _peer_route(x, mod, mod_row, gain, wq_bf16, keys)
    return _peer_experts(ht, u, vt_bf16, rank2, e2, n_sel, e1, x, mod, mod_row)


def _pad_heads(w, heads, hd):
    lead = w.shape[:-1]
    w = w.reshape(lead + (heads, hd))
    w = jnp.pad(w, [(0, 0)] * len(lead) + [(0, 0), (0, LANES - hd)])
    return w.reshape(lead + (heads * LANES,))


def _rope_tables(seq):
    half = ATT_HD // 2
    inv_freq = 1.0 / (ROPE_THETA ** (np.arange(0, half, 2, dtype=np.float32) / half))
    pos = np.arange(seq)
    lane = np.arange(LANES)
    axis_pos = np.where((lane[None, :] % ATT_HD) < half, (pos // GRID_W)[:, None], (pos % GRID_W)[:, None])
    ang = axis_pos.astype(np.float32) * inv_freq[lane % (half // 2)][None, :]
    real = (lane < ATT_HD)[None, :]
    cos = np.where(real, np.cos(ang), 1.0).astype(np.float32)
    sign = np.where((lane % half) < half // 2, -1.0, 1.0)[None, :]
    sin = np.where(real, np.sin(ang) * sign, 0.0).astype(np.float32)
    return jnp.asarray(cos), jnp.asarray(sin)


def kernel(x_prompt, x_sample, state_l0_gla, cache_l1_k, cache_l1_v, c, c_ctx, l0_w_mod, l0_b_mod, l0_norm1, l0_w_in, l0_w_alpha_f, l0_b_alpha_f, l0_w_alpha_b, l0_b_alpha_b, l0_gla_norm, l0_pool_w, l0_pool_scale, l0_w_out, l0_norm2, l0_peer_wq, l0_peer_keys, l0_peer_u, l0_peer_v, l1_w_mod, l1_b_mod, l1_norm1, l1_w_in, l1_q_norm, l1_k_norm, l1_sink, l1_w_out, l1_norm2, l1_peer_wq, l1_peer_keys, l1_peer_u, l1_peer_v):
    bp, sp, d = x_prompt.shape
    bs, ss, _ = x_sample.shape
    past = cache_l1_k.shape[1]
    xp = x_prompt.reshape(bp * sp, d)
    xs = x_sample.reshape(bs * ss, d)
    mod_p = lambda tok: 0
    mod_s = lambda tok: 4 + tok // MOD_GROUP
    cond8 = jnp.concatenate([jnp.broadcast_to(c_ctx[None, :], (4, d)), c], axis=0)
    mod0 = _modulation(cond8, l0_w_mod, l0_b_mod)
    mod1 = _modulation(cond8, l1_w_mod, l1_b_mod)

    qk, gv = GLA_HEADS * GLA_DK, GLA_HEADS * GLA_DV
    w0 = l0_w_in
    za_w = jnp.pad(w0[:, 2 * qk + 2 * gv:2 * qk + 2 * gv + 32], ((0, 0), (0, LANES - 32)))
    w0p = jnp.concatenate([_pad_heads(w0[:, :qk], GLA_HEADS, GLA_DK),
                           _pad_heads(w0[:, qk:2 * qk], GLA_HEADS, GLA_DK),
                           w0[:, 2 * qk:2 * qk + gv], w0[:, 2 * qk + gv:2 * qk + 2 * gv],
                           w0[:, 2 * qk + 2 * gv + 32:], za_w], axis=1).astype(BF16)
    splits0 = (GLA_HEADS * LANES, GLA_HEADS * LANES, gv, gv, 512, LANES)
    waf = jnp.pad(_pad_heads(l0_w_alpha_f, GLA_HEADS, GLA_DK), ((0, LANES - 16), (0, 0))).astype(BF16)
    wab = jnp.pad(_pad_heads(l0_w_alpha_b, GLA_HEADS, GLA_DK), ((16, LANES - 32), (0, 0))).astype(BF16)
    baf = _pad_heads(l0_b_alpha_f, GLA_HEADS, GLA_DK).reshape(1, -1)
    bab = _pad_heads(l0_b_alpha_b, GLA_HEADS, GLA_DK).reshape(1, -1)
    gnorm = l0_gla_norm.reshape(1, GLA_DV)
    pool_w = l0_pool_w.astype(BF16)
    pool_scale = l0_pool_scale.reshape(1, -1)
    w_out0 = l0_w_out.astype(BF16)

    def layer0(x, mod_row, seq, s0):
        qp, kp, v, r, pin, za = _in_proj(x, mod0, mod_row, l0_norm1, w0p, splits0)
        go, s_fin = _gla(qp, kp, v, r, za, s0, waf, baf, wab, bab, gnorm, seq=seq, row0=0)
        x = _mix0_out(go, pin, x, mod0, mod_row, pool_w, pool_scale, w_out0, seq=seq)
        x = _peer(x, mod0, mod_row, l0_norm2, l0_peer_wq.astype(BF16), l0_peer_keys, l0_peer_u,
                  l0_peer_v.T.astype(BF16))
        return x, s_fin

    xp, new_state = layer0(xp, mod_p, sp, jnp.zeros((bp,) + state_l0_gla.shape[1:], F32))
    xs, _ = layer0(xs, mod_s, ss, state_l0_gla)

    nq, nkv = ATT_HEADS * ATT_HD, ATT_KV_HEADS * ATT_HD
    w1 = l1_w_in
    w1p = jnp.concatenate([_pad_heads(w1[:, :nq], ATT_HEADS, ATT_HD),
                           _pad_heads(w1[:, nq:nq + nkv], ATT_KV_HEADS, ATT_HD),
                           _pad_heads(w1[:, nq + nkv:], ATT_KV_HEADS, ATT_HD)], axis=1).astype(BF16)
    splits1 = (ATT_HEADS * LANES, ATT_KV_HEADS * LANES, ATT_KV_HEADS * LANES)
    q_gain = jnp.pad(l1_q_norm, (0, LANES - ATT_HD)).reshape(1, LANES)
    k_gain = jnp.pad(l1_k_norm, (0, LANES - ATT_HD)).reshape(1, LANES)
    w_out1 = jnp.pad(l1_w_out.reshape(ATT_HEADS, ATT_HD, d), ((0, 0), (0, LANES - ATT_HD), (0, 0)))
    w_out1 = w_out1.reshape(ATT_HEADS * LANES, d).astype(BF16)
    peer1 = (l1_norm2, l1_peer_wq.astype(BF16), l1_peer_keys, l1_peer_u, l1_peer_v.T.astype(BF16))

    q, k, v = _in_proj(xp, mod1, mod_p, l1_norm1, w1p, splits1)
    xp, k_normed = _ctx_attention(q, k, v, xp, mod1, q_gain, k_gain, l1_sink, w_out1, seq=sp)
    new_k = k_normed.reshape(bp, sp, ATT_KV_HEADS, LANES)[..., :ATT_HD]
    new_v = v.reshape(bp, sp, ATT_KV_HEADS, LANES)[..., :ATT_HD]
    xp = _peer(xp, mod1, mod_p, *peer1)

    q, k, v = _in_proj(xs, mod1, mod_s, l1_norm1, w1p, splits1)
    kc = _pad_heads(cache_l1_k.reshape(bs * past, nkv), ATT_KV_HEADS, ATT_HD)
    vc = _pad_heads(cache_l1_v.reshape(bs * past, nkv), ATT_KV_HEADS, ATT_HD)
    cos, sin_signed = _rope_tables(ss)
    xs = _win_attention(q, k, v, kc, vc, cos, sin_signed, xs, mod1, q_gain, k_gain, l1_sink, w_out1, seq=ss)
    xs = _peer(xs, mod1, mod_s, *peer1)

    return (xp.reshape(bp, sp, d), xs.reshape(bs, ss, d), new_state, new_k, new_v)
```

```python
import functools

import numpy as np
import jax
import jax.numpy as jnp
from jax import lax
from jax.experimental import pallas as pl
from jax.experimental.pallas import tpu as pltpu

F32 = jnp.float32
BF16 = jnp.bfloat16

D_MODEL = 1024
EPS = 1e-6
LANES = 128
MOD_GROUP = 1024
VMEM_LIMIT = 56 * 1024 * 1024

GLA_HEADS = 4
GLA_DK = 64
GLA_DV = 128
GLA_TAU = 16.0
GLA_CHUNK = 64
POOL_WINDOWS = (2, 4, 8, 16)
POOL_GROUP_DIM = 128

ATT_HEADS = 16
ATT_KV_HEADS = 4
ATT_HD = 64
ATT_GROUP = ATT_HEADS // ATT_KV_HEADS
ATT_WINDOW = 128
GRID_W = 64
ROPE_THETA = 10000.0

PEER_HEADS = 8
PEER_NKEYS = 128
PEER_TOPK = 16
PEER_HALF = 128
NEG = -3.0e38


def _dot(a, b):
    return jnp.dot(a, b, preferred_element_type=F32)


def _dot_nt(a, b):
    return lax.dot_general(a, b, (((1,), (1,)), ((), ())), preferred_element_type=F32)


def _dot_tn(a, b):
    return lax.dot_general(a, b, (((0,), (0,)), ((), ())), preferred_element_type=F32)


def _dot_hi(a, b):
    return jnp.dot(a, b, precision=lax.Precision.HIGHEST, preferred_element_type=F32)


def _silu(x):
    return x / (1.0 + jnp.exp(-x))


def _log_sigmoid(x):
    return jnp.minimum(x, 0.0) - jnp.log(1.0 + jnp.exp(-jnp.abs(x)))


def _params(*sem):
    return pltpu.CompilerParams(dimension_semantics=sem, vmem_limit_bytes=VMEM_LIMIT)


def _mod_kernel(cond_ref, w_ref, b_ref, o_ref):
    o_ref[...] = _dot_hi(_silu(cond_ref[...]), w_ref[...]) + b_ref[...]


def _modulation(cond8, w_mod, b_mod):
    n = w_mod.shape[1]
    tn = 1024
    out = pl.pallas_call(
        _mod_kernel,
        grid=(n // tn,),
        in_specs=[pl.BlockSpec((8, D_MODEL), lambda j: (0, 0)),
                  pl.BlockSpec((D_MODEL, tn), lambda j: (0, j)),
                  pl.BlockSpec((1, tn), lambda j: (0, j))],
        out_specs=pl.BlockSpec((8, tn), lambda j: (0, j)),
        out_shape=jax.ShapeDtypeStruct((8, n), F32),
        compiler_params=_params("arbitrary"),
        name="modulation",
    )(cond8, w_mod, b_mod.reshape(1, n))
    return out.reshape(8, 6, D_MODEL)


def _modulated(x, gain, mod_ref, shift_idx):
    ms = jnp.mean(x * x, axis=-1, keepdims=True)
    y = x * lax.rsqrt(ms + EPS) * gain
    return y * (1.0 + mod_ref[0, shift_idx + 1:shift_idx + 2, :]) + mod_ref[0, shift_idx:shift_idx + 1, :]


def _proj_kernel(x_ref, mod_ref, gain_ref, w_ref, *o_refs, splits):
    hb = _modulated(x_ref[...], gain_ref[...], mod_ref, 0).astype(BF16)
    off = 0
    for o_ref, n in zip(o_refs, splits):
        o_ref[...] = _dot(hb, w_ref[:, off:off + n])
        off += n


def _in_proj(x, mod, mod_row, gain, w_bf16, splits, tm=512):
    t = x.shape[0]
    ntot = w_bf16.shape[1]
    return pl.pallas_call(
        functools.partial(_proj_kernel, splits=splits),
        grid=(t // tm,),
        in_specs=[pl.BlockSpec((tm, D_MODEL), lambda i: (i, 0)),
                  pl.BlockSpec((1, 6, D_MODEL), lambda i: (mod_row(i * tm), 0, 0)),
                  pl.BlockSpec((1, D_MODEL), lambda i: (0, 0)),
                  pl.BlockSpec((D_MODEL, ntot), lambda i: (0, 0))],
        out_specs=[pl.BlockSpec((tm, n), lambda i: (i, 0)) for n in splits],
        out_shape=[jax.ShapeDtypeStruct((t, n), F32) for n in splits],
        compiler_params=_params("arbitrary"),
        name="in_proj",
    )(x, mod, gain.reshape(1, D_MODEL), w_bf16)


def _gla_direction(q, k, v, la, state_t, o_ref, *, seq, reverse, accumulate):
    c = GLA_CHUNK
    n = seq // c
    row = lax.broadcasted_iota(jnp.int32, (c, c), 0)
    col = lax.broadcasted_iota(jnp.int32, (c, c), 1)
    keep = (row <= col) if reverse else (row >= col)
    tri = keep.astype(F32)
    order = range(n - 1, -1, -1) if reverse else range(n)
    last = 0 if reverse else c - 1
    for ci in order:
        sl = slice(ci * c, (ci + 1) * c)
        qc, kc, vc, lac = q[sl], k[sl], v[sl], la[sl]
        b = _dot_hi(tri, lac)
        btot = b[last:last + 1]
        bmid = b[c // 2 - 1:c // 2]
        qm = (qc * jnp.exp(b - bmid)).astype(BF16)
        km = (kc * jnp.exp(bmid - b)).astype(BF16)
        scores = jnp.where(keep, _dot_nt(qm, km), 0.0)
        o = _dot(scores.astype(BF16), vc.astype(BF16))
        q_in = (qc * jnp.exp(b)).astype(BF16)
        k_up = (kc * jnp.exp(btot - b)).astype(BF16)
        o = o + _dot_nt(q_in, state_t.astype(BF16))
        state_t = state_t * jnp.exp(btot) + _dot_tn(vc.astype(BF16), k_up)
        if accumulate:
            o_ref[sl, :] = o_ref[sl, :] + o
        else:
            o_ref[sl, :] = o
    return state_t


def _gla_kernel(q_ref, k_ref, v_ref, r_ref, za_ref, s0_ref, waf_ref, baf_ref, wab_ref, bab_ref, gn_ref,
                go_ref, sfin_ref, o_scr, *, seq):
    q = q_ref[...] * (GLA_DK ** -0.5)
    k = k_ref[...]
    v = v_ref[...]
    zab = za_ref[...].astype(BF16)
    la_f = _log_sigmoid(_dot(zab, waf_ref[...]) + baf_ref[...]) * (1.0 / GLA_TAU)
    la_b = _log_sigmoid(_dot(zab, wab_ref[...]) + bab_ref[...]) * (1.0 / GLA_TAU)
    pad = jnp.zeros((LANES - GLA_DK, GLA_DV), F32)
    for d, la in ((0, la_f), (1, la_b)):
        s0_t = jnp.concatenate([s0_ref[0, d, 0], pad], axis=0).T
        s_t = _gla_direction(q, k, v, la, s0_t, o_scr, seq=seq, reverse=(d == 1), accumulate=(d == 1))
        sfin_ref[0, d, 0] = s_t.T[:GLA_DK, :]
    o = o_scr[...]
    o = o * lax.rsqrt(jnp.mean(o * o, axis=-1, keepdims=True) + EPS) * gn_ref[...]
    go_ref[...] = o * _silu(r_ref[...])


def _gla(qp, kp, v, r, za, s0, waf, baf, wab, bab, gla_norm, *, seq, row0):
    nseq = s0.shape[0]
    head_blk = lambda b, h: (row0 + b, h)
    st_blk = pl.BlockSpec((1, 2, 1, GLA_DK, GLA_DV), lambda b, h: (b, 0, h, 0, 0))
    w_blk = pl.BlockSpec((LANES, LANES), lambda b, h: (0, h))
    b_blk = pl.BlockSpec((1, LANES), lambda b, h: (0, h))
    return pl.pallas_call(
        functools.partial(_gla_kernel, seq=seq),
        grid=(nseq, GLA_HEADS),
        in_specs=[pl.BlockSpec((seq, LANES), head_blk)] * 4
        + [pl.BlockSpec((seq, LANES), lambda b, h: (row0 + b, 0)), st_blk, w_blk, b_blk, w_blk, b_blk,
           pl.BlockSpec((1, LANES), lambda b, h: (0, 0))],
        out_specs=[pl.BlockSpec((seq, LANES), lambda b, h: (b, h)), st_blk],
        out_shape=[jax.ShapeDtypeStruct((nseq * seq, GLA_HEADS * GLA_DV), F32),
                   jax.ShapeDtypeStruct((nseq, 2, GLA_HEADS, GLA_DK, GLA_DV), F32)],
        scratch_shapes=[pltpu.VMEM((seq, LANES), F32)],
        compiler_params=_params("arbitrary", "arbitrary"),
        name="gla",
    )(qp, kp, v, r, za, s0, waf, baf, wab, bab, gla_norm)


def _mix0_out_kernel(go_ref, pin_ref, x_ref, mod_ref, pw_ref, ps_ref, wo_ref, o_ref, *, seq, tr):
    t0 = pl.multiple_of(pl.program_id(1) * tr, tr)
    rows = t0 + lax.broadcasted_iota(jnp.int32, (tr, seq), 0)
    cols = lax.broadcasted_iota(jnp.int32, (tr, seq), 1)
    rcol = t0 + lax.broadcasted_iota(jnp.int32, (tr, 1), 0)
    y = _dot(go_ref[...].astype(BF16), wo_ref[0:GLA_HEADS * GLA_DV, :])
    for g, w in enumerate(POOL_WINDOWS):
        sl = slice(g * POOL_GROUP_DIM, (g + 1) * POOL_GROUP_DIM)
        lo = jnp.maximum(rows - w // 2, 0)
        hi = jnp.minimum(rows + w // 2, seq)
        band = jnp.where((cols >= lo) & (cols < hi), 1.0, 0.0).astype(BF16)
        pg = pin_ref[:, sl]
        pg_hi = pg.astype(BF16)
        pg_lo = (pg - pg_hi.astype(F32)).astype(BF16)
        wsum = _dot(band, pg_hi) + _dot(band, pg_lo)
        cnt = (jnp.minimum(rcol + w // 2, seq) - jnp.maximum(rcol - w // 2, 0)).astype(F32)
        dg = wsum / cnt - pin_ref[pl.ds(t0, tr), sl]
        yg = _dot(dg.astype(BF16), pw_ref[g]) * ps_ref[:, sl]
        y = y + _dot(yg.astype(BF16), wo_ref[GLA_HEADS * GLA_DV + g * POOL_GROUP_DIM:
                                             GLA_HEADS * GLA_DV + (g + 1) * POOL_GROUP_DIM, :])
    o_ref[...] = x_ref[...] + mod_ref[0, 2:3, :] * y


def _mix0_out(go, pin, x, mod, mod_row, pool_w, pool_scale, w_out, *, seq, tr=256):
    t = x.shape[0]
    nseq, nr = t // seq, seq // tr
    return pl.pallas_call(
        functools.partial(_mix0_out_kernel, seq=seq, tr=tr),
        grid=(nseq, nr),
        in_specs=[pl.BlockSpec((tr, 512), lambda b, r: (b * nr + r, 0)),
                  pl.BlockSpec((seq, 512), lambda b, r: (b, 0)),
                  pl.BlockSpec((tr, D_MODEL), lambda b, r: (b * nr + r, 0)),
                  pl.BlockSpec((1, 6, D_MODEL), lambda b, r: (mod_row(b * seq), 0, 0)),
                  pl.BlockSpec((4, POOL_GROUP_DIM, POOL_GROUP_DIM), lambda b, r: (0, 0, 0)),
                  pl.BlockSpec((1, 512), lambda b, r: (0, 0)),
                  pl.BlockSpec((D_MODEL, D_MODEL), lambda b, r: (0, 0))],
        out_specs=pl.BlockSpec((tr, D_MODEL), lambda b, r: (b * nr + r, 0)),
        out_shape=jax.ShapeDtypeStruct((t, D_MODEL), F32),
        compiler_params=_params("arbitrary", "arbitrary"),
        name="mix0_out",
    )(go, pin, x, mod, pool_w, pool_scale, w_out)


def _head_norm(x, gain):
    ms = jnp.sum(x * x, axis=-1, keepdims=True) * (1.0 / ATT_HD)
    return x * lax.rsqrt(ms + EPS) * gain


def _rope(x, cos, sin_signed):
    lane = lax.broadcasted_iota(jnp.int32, x.shape, 1)
    partner = jnp.where((lane & 31) < 16, pltpu.roll(x, LANES - 16, 1), pltpu.roll(x, 16, 1))
    return x * cos + partner * sin_signed


def _ctx_attn_kernel(q_ref, k_ref, v_ref, x_ref, mod_ref, qg_ref, kg_ref, sink_ref, wo_ref,
                     o_ref, ko_ref):
    y = jnp.zeros(o_ref.shape, F32)
    for kv in range(ATT_KV_HEADS):
        ksl = slice(kv * LANES, (kv + 1) * LANES)
        k = _head_norm(k_ref[:, ksl], kg_ref[...])
        ko_ref[:, ksl] = k
        kb = k.astype(BF16)
        vb = v_ref[:, ksl].astype(BF16)
        for g in range(ATT_GROUP):
            h = kv * ATT_GROUP + g
            hsl = slice(h * LANES, (h + 1) * LANES)
            q = _head_norm(q_ref[:, hsl], qg_ref[...]) * (ATT_HD ** -0.5)
            s = _dot_nt(q.astype(BF16), kb)
            sink = sink_ref[h]
            m = jnp.maximum(jnp.max(s, axis=-1, keepdims=True), sink)
            p = jnp.exp(s - m)
            denom = jnp.sum(p, axis=-1, keepdims=True) + jnp.exp(sink - m)
            o = _dot(p.astype(BF16), vb) / denom
            y = y + _dot(o.astype(BF16), wo_ref[hsl, :])
    o_ref[...] = x_ref[...] + mod_ref[0, 2:3, :] * y


def _ctx_attention(q, k, v, x, mod, q_gain, k_gain, sink, w_out_p, *, seq):
    t = x.shape[0]
    nseq = t // seq
    row = lambda b: (b, 0)
    return pl.pallas_call(
        _ctx_attn_kernel,
        grid=(nseq,),
        in_specs=[pl.BlockSpec((seq, ATT_HEADS * LANES), row),
                  pl.BlockSpec((seq, ATT_KV_HEADS * LANES), row),
                  pl.BlockSpec((seq, ATT_KV_HEADS * LANES), row),
                  pl.BlockSpec((seq, D_MODEL), row),
                  pl.BlockSpec((1, 6, D_MODEL), lambda b: (0, 0, 0)),
                  pl.BlockSpec((1, LANES), lambda b: (0, 0)),
                  pl.BlockSpec((1, LANES), lambda b: (0, 0)),
                  pl.BlockSpec(memory_space=pltpu.SMEM),
                  pl.BlockSpec((ATT_HEADS * LANES, D_MODEL), lambda b: (0, 0))],
        out_specs=[pl.BlockSpec((seq, D_MODEL), row),
                   pl.BlockSpec((seq, ATT_KV_HEADS * LANES), row)],
        out_shape=[jax.ShapeDtypeStruct((t, D_MODEL), F32),
                   jax.ShapeDtypeStruct((t, ATT_KV_HEADS * LANES), F32)],
        compiler_params=_params("arbitrary"),
        name="ctx_attention",
    )(q, k, v, x, mod, q_gain, k_gain, sink, w_out_p)


def _win_attn_kernel(q_ref, k_ref, v_ref, kc_ref, vc_ref, cq_ref, sq_ref, ck_ref, sk_ref, x_ref, mod_ref,
                     qg_ref, kg_ref, sink_ref, wo_ref, o_ref, kr_scr, *, seq, tq):
    i = pl.program_id(1)
    nb = seq // tq

    @pl.when(i == 0)
    def _():
        for kv in range(ATT_KV_HEADS):
            ksl = slice(kv * LANES, (kv + 1) * LANES)
            kr_scr[:, ksl] = _rope(_head_norm(k_ref[:, ksl], kg_ref[...]), ck_ref[...], sk_ref[...])

    row = lax.broadcasted_iota(jnp.int32, (tq, tq), 0)
    col = lax.broadcasted_iota(jnp.int32, (tq, tq), 1)
    band = {-1: col >= row, 0: None, 1: col <= row}
    y = jnp.zeros(o_ref.shape, F32)
    for kv in range(ATT_KV_HEADS):
        ksl = slice(kv * LANES, (kv + 1) * LANES)
        kcb = kc_ref[:, ksl].astype(BF16)
        vcb = vc_ref[:, ksl].astype(BF16)
        kloc, vloc = {}, {}
        for j in (-1, 0, 1):
            start = pl.multiple_of(jnp.clip(i + j, 0, nb - 1) * tq, tq)
            kloc[j] = kr_scr[pl.ds(start, tq), ksl].astype(BF16)
            vloc[j] = v_ref[pl.ds(start, tq), ksl].astype(BF16)
        for g in range(ATT_GROUP):
            h = kv * ATT_GROUP + g
            hsl = slice(h * LANES, (h + 1) * LANES)
            q = _rope(_head_norm(q_ref[:, hsl], qg_ref[...]), cq_ref[...], sq_ref[...]) * (ATT_HD ** -0.5)
            qb = q.astype(BF16)
            sink = sink_ref[h]
            s_loc = {}
            for j in (-1, 0, 1):
                s = _dot_nt(qb, kloc[j])
                if j != 0:
                    limit = jnp.where((i + j >= 0) & (i + j < nb), -NEG, NEG)
                    s = jnp.minimum(jnp.where(band[j], s, NEG), limit)
                s_loc[j] = s
            s_ctx = _dot_nt(qb, kcb)
            m = jnp.maximum(jnp.max(s_ctx, axis=-1, keepdims=True), sink)
            for j in (-1, 0, 1):
                m = jnp.maximum(m, jnp.max(s_loc[j], axis=-1, keepdims=True))
            p_ctx = jnp.exp(s_ctx - m)
            denom = jnp.sum(p_ctx, axis=-1, keepdims=True) + jnp.exp(sink - m)
            o = _dot(p_ctx.astype(BF16), vcb)
            for j in (-1, 0, 1):
                p = jnp.exp(s_loc[j] - m)
                denom = denom + jnp.sum(p, axis=-1, keepdims=True)
                o = o + _dot(p.astype(BF16), vloc[j])
            y = y + _dot((o / denom).astype(BF16), wo_ref[hsl, :])
    o_ref[...] = x_ref[...] + mod_ref[0, 2:3, :] * y


def _win_attention(q, k, v, kc, vc, cos, sin_signed, x, mod, q_gain, k_gain, sink, w_out_p, *, seq, tq=128):
    t = x.shape[0]
    nseq, nb = t // seq, seq // tq
    ctx = kc.shape[0] // nseq
    qrow = lambda b, i: (b * nb + i, 0)
    srow = lambda b, i: (b, 0)
    one = lambda b, i: (0, 0)
    return pl.pallas_call(
        functools.partial(_win_attn_kernel, seq=seq, tq=tq),
        grid=(nseq, nb),
        in_specs=[pl.BlockSpec((tq, ATT_HEADS * LANES), qrow),
                  pl.BlockSpec((seq, ATT_KV_HEADS * LANES), srow),
                  pl.BlockSpec((seq, ATT_KV_HEADS * LANES), srow),
                  pl.BlockSpec((ctx, ATT_KV_HEADS * LANES), srow),
                  pl.BlockSpec((ctx, ATT_KV_HEADS * LANES), srow),
                  pl.BlockSpec((tq, LANES), lambda b, i: (i, 0)),
                  pl.BlockSpec((tq, LANES), lambda b, i: (i, 0)),
                  pl.BlockSpec((seq, LANES), one),
                  pl.BlockSpec((seq, LANES), one),
                  pl.BlockSpec((tq, D_MODEL), qrow),
                  pl.BlockSpec((1, 6, D_MODEL), lambda b, i: (4 + b, 0, 0)),
                  pl.BlockSpec((1, LANES), one),
                  pl.BlockSpec((1, LANES), one),
                  pl.BlockSpec(memory_space=pltpu.SMEM),
                  pl.BlockSpec((ATT_HEADS * LANES, D_MODEL), one)],
        out_specs=pl.BlockSpec((tq, D_MODEL), qrow),
        out_shape=jax.ShapeDtypeStruct((t, D_MODEL), F32),
        scratch_shapes=[pltpu.VMEM((seq, ATT_KV_HEADS * LANES), F32)],
        compiler_params=_params("arbitrary", "arbitrary"),
        name="win_attention",
    )(q, k, v, kc, vc, cos, sin_signed, cos, sin_signed, x, mod, q_gain, k_gain, sink, w_out_p)


def _top16(s, vals_ref, slot, exact_ties):
    key = lax.broadcasted_iota(jnp.int32, s.shape, 0).astype(F32)
    work = s
    rank = jnp.full(s.shape, float(PEER_NKEYS - 1), F32)
    for kk in range(PEER_TOPK):
        m = jnp.max(work, axis=0, keepdims=True)
        hit = work == m
        if exact_ties:
            hit = key == jnp.min(jnp.where(hit, key, float(PEER_NKEYS)), axis=0, keepdims=True)
        rank = jnp.where(hit, float(kk), rank)
        work = jnp.where(hit, NEG, work)
        vals_ref[slot, kk:kk + 1, :] = m
    return rank


def _route_select(s1, s2, vals_scr, exact_ties):
    rank1 = _top16(s1, vals_scr, 0, exact_ties)
    rank2 = _top16(s2, vals_scr, 1, exact_ties)
    a = vals_scr[0]
    b = vals_scr[1]
    blocks = [a[0:1] + b]
    blocks += [a[p:p + 1] + b[0:8] for p in range(1, 8)]
    blocks += [a[8:16] + b[0:1]]
    cand = jnp.concatenate(blocks, axis=0)
    r = lax.broadcasted_iota(jnp.int32, cand.shape, 0)
    p_idx = jnp.where(r < 16, 0, jnp.where(r < 72, ((r - 16) >> 3) + 1, r - 64))
    q_idx = jnp.where(r < 16, r, jnp.where(r < 72, (r - 16) & 7, 0))
    flat = (p_idx * PEER_TOPK + q_idx).astype(F32)
    work = jnp.where((p_idx + 1) * (q_idx + 1) <= PEER_TOPK, cand, NEG)
    chosen = jnp.zeros(cand.shape, F32)
    for _ in range(PEER_TOPK):
        hit = work == jnp.max(work, axis=0, keepdims=True)
        if exact_ties:
            first = jnp.min(jnp.where(hit, flat, float(PEER_TOPK * PEER_TOPK)), axis=0, keepdims=True)
            hit = flat == first
        chosen = jnp.where(hit, 1.0, chosen)
        work = jnp.where(hit, NEG, work)
    z = jnp.sum(chosen * jnp.exp(cand - cand[0:1]), axis=0, keepdims=True)
    counts = [jnp.sum(chosen[0:16], axis=0, keepdims=True)]
    counts += [jnp.sum(chosen[8 + 8 * p:16 + 8 * p], axis=0, keepdims=True) for p in range(1, 8)]
    counts += [chosen[64 + p:65 + p] for p in range(8, 16)]
    n_sel = jnp.zeros(s1.shape, F32)
    for p in range(PEER_TOPK):
        n_sel = jnp.where(rank1 == float(p), counts[p], n_sel)
    picked = (jnp.sum(jnp.where(rank1 < float(PEER_TOPK), 1.0, 0.0), axis=0, keepdims=True)
              + jnp.sum(jnp.where(rank2 < float(PEER_TOPK), 1.0, 0.0), axis=0, keepdims=True)
              + jnp.sum(chosen, axis=0, keepdims=True))
    excess = picked - float(3 * PEER_TOPK)
    return rank2, n_sel, jnp.exp(s1 - a[0:1]) / z, jnp.exp(s2 - b[0:1]), excess


def _route_kernel(x_ref, mod_ref, gain_ref, wq_ref, keys_ref, ht_ref, rank2_ref, e2_ref, n_ref, e1_ref,
                  hb_scr, vals_scr):
    @pl.when(pl.program_id(1) == 0)
    def _():
        h = _modulated(x_ref[...], gain_ref[...], mod_ref, 3)
        hb_scr[...] = h.astype(BF16)
        ht_ref[...] = h.T.astype(BF16)

    q = _dot(hb_scr[...], wq_ref[...])
    s1 = _dot_nt(keys_ref[0, 0].astype(BF16), q[:, :PEER_HALF].astype(BF16))
    s2 = _dot_nt(keys_ref[0, 1].astype(BF16), q[:, PEER_HALF:].astype(BF16))

    def select(exact_ties):
        rank2, n_sel, e1, e2, excess = _route_select(s1, s2, vals_scr, exact_ties)
        rank2_ref[0] = rank2.astype(rank2_ref.dtype)
        n_ref[0] = n_sel
        e1_ref[0] = e1
        e2_ref[0] = e2.astype(e2_ref.dtype)
        return excess

    excess = select(False)
    tied = jnp.max(excess, axis=1, keepdims=True)[0, 0] > 0.0

    @pl.when(tied)
    def _():
        select(True)


def _peer_route(x, mod, mod_row, gain, wq_bf16, keys, tm=512):
    t = x.shape[0]
    route = jax.ShapeDtypeStruct((PEER_HEADS, PEER_NKEYS, t), F32)
    route_b = jax.ShapeDtypeStruct((PEER_HEADS, PEER_NKEYS, t), BF16)
    rblk = pl.BlockSpec((1, PEER_NKEYS, tm), lambda i, h: (h, 0, i))
    return pl.pallas_call(
        _route_kernel,
        grid=(t // tm, PEER_HEADS),
        in_specs=[pl.BlockSpec((tm, D_MODEL), lambda i, h: (i, 0)),
                  pl.BlockSpec((1, 6, D_MODEL), lambda i, h: (mod_row(i * tm), 0, 0)),
                  pl.BlockSpec((1, D_MODEL), lambda i, h: (0, 0)),
                  pl.BlockSpec((D_MODEL, 2 * PEER_HALF), lambda i, h: (0, h)),
                  pl.BlockSpec((1, 2, PEER_NKEYS, PEER_HALF), lambda i, h: (h, 0, 0, 0))],
        out_specs=[pl.BlockSpec((D_MODEL, tm), lambda i, h: (0, i)), rblk, rblk, rblk, rblk],
        out_shape=[jax.ShapeDtypeStruct((D_MODEL, t), BF16), route_b, route_b, route, route],
        scratch_shapes=[pltpu.VMEM((tm, D_MODEL), BF16), pltpu.VMEM((2, PEER_TOPK, tm), F32)],
        compiler_params=_params("arbitrary", "arbitrary"),
        name="peer_route",
    )(x, mod, gain.reshape(1, D_MODEL), wq_bf16, keys)


def _gelu_tanh(x):
    return 0.5 * x * (1.0 + jnp.tanh(0.7978845608028654 * (x + 0.044715 * (x * x * x))))


def _expert_kernel(ht_ref, u_ref, vt_ref, rank2_ref, e2_ref, n_ref, e1_ref, x_ref, mod_ref, o_ref, acc_scr,
                   a0_scr, a1_scr, wt_scr, rank2_scr, e2_scr, *, ic, tok_blk):
    s = pl.program_id(1)
    tt = ht_ref.shape[1]

    @pl.when(s == 0)
    def _():
        acc_scr[...] = jnp.zeros(acc_scr.shape, F32)
        a1_scr[...] = jnp.zeros(a1_scr.shape, F32)
        for h in range(PEER_HEADS):
            rank2_scr[h] = rank2_ref[h].astype(BF16)
            e2_scr[h] = e2_ref[h].astype(BF16)

    def spread(row):
        return jnp.broadcast_to(row, (PEER_NKEYS, row.shape[1])).astype(BF16)

    c = jnp.maximum(s - 1, 0)

    def step(a_new, a_old):
        a_new[...] = _dot(u_ref[...], ht_ref[...])
        for pair in range(ic // 2):
            for ii in (2 * pair, 2 * pair + 1):
                i = c * ic + ii
                esl = slice(ii * PEER_NKEYS, (ii + 1) * PEER_NKEYS)
                n_rows = [n_ref[h, pl.ds(i, 1), :] for h in range(PEER_HEADS)]
                e1_rows = [e1_ref[h, pl.ds(i, 1), :] for h in range(PEER_HEADS)]
                for tb in range(tt // tok_blk):
                    tsl = slice(tb * tok_blk, (tb + 1) * tok_blk)
                    gate = None
                    for h in range(PEER_HEADS):
                        term = (jnp.where(rank2_scr[h, :, tsl] < spread(n_rows[h][:, tsl]),
                                          e2_scr[h, :, tsl], 0.0) * spread(e1_rows[h][:, tsl]))
                        gate = term if gate is None else gate + term
                    wt_scr[esl, tsl] = _gelu_tanh(a_old[esl, tsl]).astype(BF16) * gate
            psl = slice(2 * pair * PEER_NKEYS, (2 * pair + 2) * PEER_NKEYS)
            acc_scr[...] += _dot(vt_ref[:, psl], wt_scr[psl, :])

    @pl.when(s % 2 == 0)
    def _():
        step(a0_scr, a1_scr)

    @pl.when(s % 2 == 1)
    def _():
        step(a1_scr, a0_scr)

    @pl.when(s == pl.num_programs(1) - 1)
    def _():
        o_ref[...] = x_ref[...] + mod_ref[0, 5:6, :] * acc_scr[...].T


def _peer_experts(ht, u_bf16, vt_bf16, rank2, e2, n_sel, e1, x, mod, mod_row, tt=512, ic=8, tok_blk=128):
    t = x.shape[0]
    ec = ic * PEER_NKEYS
    nchunk = PEER_NKEYS // ic
    rblk = pl.BlockSpec((PEER_HEADS, PEER_NKEYS, tt), lambda i, s: (0, 0, i))
    return pl.pallas_call(
        functools.partial(_expert_kernel, ic=ic, tok_blk=tok_blk),
        grid=(t // tt, nchunk + 1),
        in_specs=[pl.BlockSpec((D_MODEL, tt), lambda i, s: (0, i)),
                  pl.BlockSpec((ec, D_MODEL), lambda i, s: (jnp.minimum(s, nchunk - 1), 0)),
                  pl.BlockSpec((D_MODEL, ec), lambda i, s: (0, jnp.maximum(s - 1, 0))),
                  rblk, rblk, rblk, rblk,
                  pl.BlockSpec((tt, D_MODEL), lambda i, s: (i, 0)),
                  pl.BlockSpec((1, 6, D_MODEL), lambda i, s: (mod_row(i * tt), 0, 0))],
        out_specs=pl.BlockSpec((tt, D_MODEL), lambda i, s: (i, 0)),
        out_shape=jax.ShapeDtypeStruct((t, D_MODEL), F32),
        scratch_shapes=[pltpu.VMEM((D_MODEL, tt), F32), pltpu.VMEM((ec, tt), F32), pltpu.VMEM((ec, tt), F32),
                        pltpu.VMEM((ec, tt), BF16),
                        pltpu.VMEM((PEER_HEADS, PEER_NKEYS, tt), BF16),
                        pltpu.VMEM((PEER_HEADS, PEER_NKEYS, tt), BF16)],
        compiler_params=_params("arbitrary", "arbitrary"),
        name="peer_experts",
    )(ht, u_bf16, vt_bf16, rank2, e2, n_sel, e1, x, mod)


def _peer(x, mod, mod_row, gain, wq_bf16, keys, u_bf16, vt_bf16):
    ht, rank2, e2, n_sel, e1 = _peer_route(x, mod, mod_row, gain, wq_bf16, keys)
    return _peer_experts(ht, u_bf16, vt_bf16, rank2, e2, n_sel, e1, x, mod, mod_row)


def _pad_heads(w, heads, hd):
    lead = w.shape[:-1]
    w = w.reshape(lead + (heads, hd))
    w = jnp.pad(w, [(0, 0)] * len(lead) + [(0, 0), (0, LANES - hd)])
    return w.reshape(lead + (heads * LANES,))


def _rope_tables(seq):
    half = ATT_HD // 2
    inv_freq = 1.0 / (ROPE_THETA ** (np.arange(0, half, 2, dtype=np.float32) / half))
    pos = np.arange(seq)
    lane = np.arange(LANES)
    axis_pos = np.where((lane[None, :] % ATT_HD) < half, (pos // GRID_W)[:, None], (pos % GRID_W)[:, None])
    ang = axis_pos.astype(np.float32) * inv_freq[lane % (half // 2)][None, :]
    real = (lane < ATT_HD)[None, :]
    cos = np.where(real, np.cos(ang), 1.0).astype(np.float32)
    sign = np.where((lane % half) < half // 2, -1.0, 1.0)[None, :]
    sin = np.where(real, np.sin(ang) * sign, 0.0).astype(np.float32)
    return jnp.asarray(cos), jnp.asarray(sin)


def kernel(x_prompt, x_sample, state_l0_gla, cache_l1_k, cache_l1_v, c, c_ctx, l0_w_mod, l0_b_mod, l0_norm1, l0_w_in, l0_w_alpha_f, l0_b_alpha_f, l0_w_alpha_b, l0_b_alpha_b, l0_gla_norm, l0_pool_w, l0_pool_scale, l0_w_out, l0_norm2, l0_peer_wq, l0_peer_keys, l0_peer_u, l0_peer_v, l1_w_mod, l1_b_mod, l1_norm1, l1_w_in, l1_q_norm, l1_k_norm, l1_sink, l1_w_out, l1_norm2, l1_peer_wq, l1_peer_keys, l1_peer_u, l1_peer_v):
    bp, sp, d = x_prompt.shape
    bs, ss, _ = x_sample.shape
    past = cache_l1_k.shape[1]
    xp = x_prompt.reshape(bp * sp, d)
    xs = x_sample.reshape(bs * ss, d)
    mod_p = lambda tok: 0
    mod_s = lambda tok: 4 + tok // MOD_GROUP
    cond8 = jnp.concatenate([jnp.broadcast_to(c_ctx[None, :], (4, d)), c], axis=0)
    mod0 = _modulation(cond8, l0_w_mod, l0_b_mod)
    mod1 = _modulation(cond8, l1_w_mod, l1_b_mod)

    qk, gv = GLA_HEADS * GLA_DK, GLA_HEADS * GLA_DV
    w0 = l0_w_in
    za_w = jnp.pad(w0[:, 2 * qk + 2 * gv:2 * qk + 2 * gv + 32], ((0, 0), (0, LANES - 32)))
    w0p = jnp.concatenate([_pad_heads(w0[:, :qk], GLA_HEADS, GLA_DK),
                           _pad_heads(w0[:, qk:2 * qk], GLA_HEADS, GLA_DK),
                           w0[:, 2 * qk:2 * qk + gv], w0[:, 2 * qk + gv:2 * qk + 2 * gv],
                           w0[:, 2 * qk + 2 * gv + 32:], za_w], axis=1).astype(BF16)
    splits0 = (GLA_HEADS * LANES, GLA_HEADS * LANES, gv, gv, 512, LANES)
    waf = jnp.pad(_pad_heads(l0_w_alpha_f, GLA_HEADS, GLA_DK), ((0, LANES - 16), (0, 0))).astype(BF16)
    wab = jnp.pad(_pad_heads(l0_w_alpha_b, GLA_HEADS, GLA_DK), ((16, LANES - 32), (0, 0))).astype(BF16)
    baf = _pad_heads(l0_b_alpha_f, GLA_HEADS, GLA_DK).reshape(1, -1)
    bab = _pad_heads(l0_b_alpha_b, GLA_HEADS, GLA_DK).reshape(1, -1)
    gnorm = l0_gla_norm.reshape(1, GLA_DV)
    pool_w = l0_pool_w.astype(BF16)
    pool_scale = l0_pool_scale.reshape(1, -1)
    w_out0 = l0_w_out.astype(BF16)

    def layer0(x, mod_row, seq, s0):
        qp, kp, v, r, pin, za = _in_proj(x, mod0, mod_row, l0_norm1, w0p, splits0)
        go, s_fin = _gla(qp, kp, v, r, za, s0, waf, baf, wab, bab, gnorm, seq=seq, row0=0)
        x = _mix0_out(go, pin, x, mod0, mod_row, pool_w, pool_scale, w_out0, seq=seq)
        x = _peer(x, mod0, mod_row, *peer0)
        return x, s_fin

    peer0 = (l0_norm2, l0_peer_wq.astype(BF16), l0_peer_keys, l0_peer_u.astype(BF16),
             l0_peer_v.T.astype(BF16))

    xp, new_state = layer0(xp, mod_p, sp, jnp.zeros((bp,) + state_l0_gla.shape[1:], F32))
    xs, _ = layer0(xs, mod_s, ss, state_l0_gla)

    nq, nkv = ATT_HEADS * ATT_HD, ATT_KV_HEADS * ATT_HD
    w1 = l1_w_in
    w1p = jnp.concatenate([_pad_heads(w1[:, :nq], ATT_HEADS, ATT_HD),
                           _pad_heads(w1[:, nq:nq + nkv], ATT_KV_HEADS, ATT_HD),
                           _pad_heads(w1[:, nq + nkv:], ATT_KV_HEADS, ATT_HD)], axis=1).astype(BF16)
    splits1 = (ATT_HEADS * LANES, ATT_KV_HEADS * LANES, ATT_KV_HEADS * LANES)
    q_gain = jnp.pad(l1_q_norm, (0, LANES - ATT_HD)).reshape(1, LANES)
    k_gain = jnp.pad(l1_k_norm, (0, LANES - ATT_HD)).reshape(1, LANES)
    w_out1 = jnp.pad(l1_w_out.reshape(ATT_HEADS, ATT_HD, d), ((0, 0), (0, LANES - ATT_HD), (0, 0)))
    w_out1 = w_out1.reshape(ATT_HEADS * LANES, d).astype(BF16)
    peer1 = (l1_norm2, l1_peer_wq.astype(BF16), l1_peer_keys, l1_peer_u.astype(BF16),
             l1_peer_v.T.astype(BF16))

    q, k, v = _in_proj(xp, mod1, mod_p, l1_norm1, w1p, splits1)
    xp, k_normed = _ctx_attention(q, k, v, xp, mod1, q_gain, k_gain, l1_sink, w_out1, seq=sp)
    new_k = k_normed.reshape(bp, sp, ATT_KV_HEADS, LANES)[..., :ATT_HD]
    new_v = v.reshape(bp, sp, ATT_KV_HEADS, LANES)[..., :ATT_HD]
    xp = _peer(xp, mod1, mod_p, *peer1)

    q, k, v = _in_proj(xs, mod1, mod_s, l1_norm1, w1p, splits1)
    kc = _pad_heads(cache_l1_k.reshape(bs * past, nkv), ATT_KV_HEADS, ATT_HD)
    vc = _pad_heads(cache_l1_v.reshape(bs * past, nkv), ATT_KV_HEADS, ATT_HD)
    cos, sin_signed = _rope_tables(ss)
    xs = _win_attention(q, k, v, kc, vc, cos, sin_signed, xs, mod1, q_gain, k_gain, l1_sink, w_out1, seq=ss)
    xs = _peer(xs, mod1, mod_s, *peer1)

    return (xp.reshape(bp, sp, d), xs.reshape(bs, ss, d), new_state, new_k, new_v)
```
